```python
import math
import jax, jax.numpy as jnp
from jax import lax
import numpy as np

D_MODEL = 1024
BATCH = 8
SEQ = 2048
DEPTH = 2
DEC_BATCH = 128
DEC_SEQ = 1
PAST_LEN = 2048
PAGE_SIZE = 128

N_MIXERS = 2
N_HEADS = 16
HEAD_DIM = D_MODEL // N_HEADS
D_FF = 2816
CONV_WIDTH = 31
D_PLE = 256
Q_BLOCK = 128
RMS_EPS = 1e-6
LN_EPS = 1e-5
LOGIT_BIAS_INIT = -7.0
N_ATTN_LAYERS = (DEPTH + 1) // 2
N_CONV_LAYERS = DEPTH // 2

kernel_name = "stickbreaking_conformer_macaron_step"


def rmsnorm(x, g):
    xf = x.astype(jnp.float32)
    y = xf * lax.rsqrt(jnp.mean(xf * xf, axis=-1, keepdims=True) + RMS_EPS) * g.astype(jnp.float32)
    return y.astype(x.dtype)


def layernorm(x, g, b):
    xf = x.astype(jnp.float32)
    mu = jnp.mean(xf, axis=-1, keepdims=True)
    var = jnp.mean(jnp.square(xf - mu), axis=-1, keepdims=True)
    y = (xf - mu) * lax.rsqrt(var + LN_EPS) * g.astype(jnp.float32) + b.astype(jnp.float32)
    return y.astype(x.dtype)


def swiglu(h, wg, wu, wd):
    return (jax.nn.silu(h @ wg) * (h @ wu)) @ wd


def qkv_project(h, w_qkv, q_gain, k_gain):
    B, T, _ = h.shape
    qkv = (h @ w_qkv).reshape(B, T, 3, N_HEADS, HEAD_DIM)
    q = rmsnorm(qkv[:, :, 0], q_gain)
    k = rmsnorm(qkv[:, :, 1], k_gain)
    return q, k, qkv[:, :, 2]


def sb_attend(q, k, v, bias, q_pos, k_pos):
    scale = 1.0 / math.sqrt(q.shape[-1])
    z = jnp.einsum('bqhd,bkhd->bhqk', q, k).astype(jnp.float32) * scale \
        + bias.astype(jnp.float32)[None, :, None, None]
    mask = k_pos[None, :] < q_pos[:, None]
    log_not = jnp.where(mask, jax.nn.log_sigmoid(-z), 0.0)
    after = lax.cumsum(log_not, axis=3, reverse=True) - log_not
    a = jnp.where(mask, jnp.exp(jax.nn.log_sigmoid(z) + after), 0.0)
    return jnp.einsum('bhqk,bkhd->bqhd', a.astype(v.dtype), v)


def sb_prompt(q, k, v, bias):
    B, T, H, Dh = q.shape
    nb = T // Q_BLOCK
    qb = q.reshape(B, nb, Q_BLOCK, H, Dh).transpose(1, 0, 2, 3, 4)
    pos = jnp.arange(T, dtype=jnp.int32)
    posb = pos.reshape(nb, Q_BLOCK)
    ob = lax.map(lambda a: sb_attend(a[0], k, v, bias, a[1], pos), (qb, posb))
    return ob.transpose(1, 0, 2, 3, 4).reshape(B, T, H, Dh)


def conv_module(h, buf, w_in, w_dw, b_dw, ln_g, ln_b, w_out):
    u = h @ w_in
    g = u[..., :D_MODEL] * jax.nn.sigmoid(u[..., D_MODEL:])
    xc = jnp.concatenate([buf.astype(g.dtype), g], axis=1)
    y = lax.conv_general_dilated(xc, w_dw[:, None, :].astype(xc.dtype), window_strides=(1,),
                                 padding='VALID', dimension_numbers=('NWC', 'WIO', 'NWC'),
                                 feature_group_count=D_MODEL) + b_dw
    y = jax.nn.silu(layernorm(y, ln_g, ln_b))
    return y @ w_out, xc[:, -(CONV_WIDTH - 1):]


def setup_inputs(seed: int = 0) -> dict:
    key = jax.random.key(seed)
    ks = iter(jax.random.split(key, 48))
    f32 = jnp.float32

    def nrm(shape, scale):
        return jax.random.normal(next(ks), shape, f32) * scale

    def gain(shape):
        return 1.0 + 0.05 * jax.random.normal(next(ks), shape, f32)

    n_pages = PAST_LEN // PAGE_SIZE
    n_used = DEC_BATCH * n_pages
    n_phys = n_used + max(1, n_used // 4)
    page_table = jax.random.permutation(next(ks), n_phys)[:n_used].reshape(DEC_BATCH, n_pages).astype(jnp.int32)

    return {
        "x_prompt": nrm((BATCH, SEQ, D_MODEL), 1.0),
        "x_sample": nrm((DEC_BATCH, DEC_SEQ, D_MODEL), 1.0),
        "cache_k": nrm((N_ATTN_LAYERS, n_phys, PAGE_SIZE, N_HEADS, HEAD_DIM), 1.0),
        "cache_v": nrm((N_ATTN_LAYERS, n_phys, PAGE_SIZE, N_HEADS, HEAD_DIM), 1.0),
        "state_conv": nrm((N_CONV_LAYERS, DEC_BATCH, CONV_WIDTH - 1, D_MODEL), 0.5),
        "page_table": page_table,
        "p_prompt": nrm((DEPTH, BATCH, SEQ, D_PLE), 1.0),
        "p_sample": nrm((DEPTH, DEC_BATCH, DEC_SEQ, D_PLE), 1.0),
        "norm_ffn1": gain((DEPTH, D_MODEL)),
        "ffn1_wg": nrm((DEPTH, D_MODEL, D_FF), D_MODEL ** -0.5),
        "ffn1_wu": nrm((DEPTH, D_MODEL, D_FF), D_MODEL ** -0.5),
        "ffn1_wd": nrm((DEPTH, D_FF, D_MODEL), D_FF ** -0.5),
        "norm_mix": gain((DEPTH, D_MODEL)),
        "attn_w_qkv": nrm((N_ATTN_LAYERS, D_MODEL, 3 * D_MODEL), D_MODEL ** -0.5),
        "attn_q_norm": gain((N_ATTN_LAYERS, HEAD_DIM)),
        "attn_k_norm": gain((N_ATTN_LAYERS, HEAD_DIM)),
        "attn_logit_bias": LOGIT_BIAS_INIT + nrm((N_ATTN_LAYERS, N_HEADS), 0.1),
        "attn_w_o": nrm((N_ATTN_LAYERS, D_MODEL, D_MODEL), D_MODEL ** -0.5),
        "conv_w_in": nrm((N_CONV_LAYERS, D_MODEL, 2 * D_MODEL), D_MODEL ** -0.5),
        "conv_w_dw": nrm((N_CONV_LAYERS, CONV_WIDTH, D_MODEL), CONV_WIDTH ** -0.5),
        "conv_b_dw": nrm((N_CONV_LAYERS, D_MODEL), 0.02),
        "conv_ln_g": gain((N_CONV_LAYERS, D_MODEL)),
        "conv_ln_b": nrm((N_CONV_LAYERS, D_MODEL), 0.02),
        "conv_w_out": nrm((N_CONV_LAYERS, D_MODEL, D_MODEL), D_MODEL ** -0.5),
        "norm_ffn2": gain((DEPTH, D_MODEL)),
        "ffn2_wg": nrm((DEPTH, D_MODEL, D_FF), D_MODEL ** -0.5),
        "ffn2_wu": nrm((DEPTH, D_MODEL, D_FF), D_MODEL ** -0.5),
        "ffn2_wd": nrm((DEPTH, D_FF, D_MODEL), D_FF ** -0.5),
        "norm_ple": gain((DEPTH, D_MODEL)),
        "ple_w_gate": nrm((DEPTH, D_MODEL, D_MODEL), D_MODEL ** -0.5),
        "ple_w_proj": nrm((DEPTH, D_PLE, D_MODEL), D_PLE ** -0.5),
    }


def reference(x_prompt, x_sample, cache_k, cache_v, state_conv, page_table, p_prompt, p_sample,
              norm_ffn1, ffn1_wg, ffn1_wu, ffn1_wd, norm_mix,
              attn_w_qkv, attn_q_norm, attn_k_norm, attn_logit_bias, attn_w_o,
              conv_w_in, conv_w_dw, conv_b_dw, conv_ln_g, conv_ln_b, conv_w_out,
              norm_ffn2, ffn2_wg, ffn2_wu, ffn2_wd, norm_ple, ple_w_gate, ple_w_proj):
    xp, xs = x_prompt, x_sample
    Bp, Tp, _ = xp.shape
    Bs, Ts, _ = xs.shape
    kp_l, vp_l, ks_l, vs_l, cp_l, cs_l = [], [], [], [], [], []
    for i in range(DEPTH):
        xp = xp + 0.5 * swiglu(rmsnorm(xp, norm_ffn1[i]), ffn1_wg[i], ffn1_wu[i], ffn1_wd[i])
        xs = xs + 0.5 * swiglu(rmsnorm(xs, norm_ffn1[i]), ffn1_wg[i], ffn1_wu[i], ffn1_wd[i])
        hp = rmsnorm(xp, norm_mix[i])
        hs = rmsnorm(xs, norm_mix[i])
        j = i // N_MIXERS
        if i % N_MIXERS == 0:
            bias = attn_logit_bias[j]
            qp, kp, vp = qkv_project(hp, attn_w_qkv[j], attn_q_norm[j], attn_k_norm[j])
            op = sb_prompt(qp, kp, vp, bias)
            qs, ks_, vs = qkv_project(hs, attn_w_qkv[j], attn_q_norm[j], attn_k_norm[j])
            past_k = cache_k[j][page_table].reshape(Bs, -1, N_HEADS, HEAD_DIM).astype(ks_.dtype)
            past_v = cache_v[j][page_table].reshape(Bs, -1, N_HEADS, HEAD_DIM).astype(vs.dtype)
            past = past_k.shape[1]
            k_all = jnp.concatenate([past_k, ks_], axis=1)
            v_all = jnp.concatenate([past_v, vs], axis=1)
            q_pos = past + jnp.arange(Ts, dtype=jnp.int32)
            k_pos = jnp.arange(past + Ts, dtype=jnp.int32)
            os_ = sb_attend(qs, k_all, v_all, bias, q_pos, k_pos)
            xp = xp + op.reshape(Bp, Tp, D_MODEL) @ attn_w_o[j]
            xs = xs + os_.reshape(Bs, Ts, D_MODEL) @ attn_w_o[j]
            kp_l.append(kp); vp_l.append(vp); ks_l.append(ks_); vs_l.append(vs)
        else:
            zero_buf = jnp.zeros((Bp, CONV_WIDTH - 1, D_MODEL), hp.dtype)
            mp, bufp = conv_module(hp, zero_buf, conv_w_in[j], conv_w_dw[j], conv_b_dw[j],
                                   conv_ln_g[j], conv_ln_b[j], conv_w_out[j])
            ms, bufs = conv_module(hs, state_conv[j], conv_w_in[j], conv_w_dw[j], conv_b_dw[j],
                                   conv_ln_g[j], conv_ln_b[j], conv_w_out[j])
            xp = xp + mp
            xs = xs + ms
            cp_l.append(bufp); cs_l.append(bufs)
        xp = xp + 0.5 * swiglu(rmsnorm(xp, norm_ffn2[i]), ffn2_wg[i], ffn2_wu[i], ffn2_wd[i])
        xs = xs + 0.5 * swiglu(rmsnorm(xs, norm_ffn2[i]), ffn2_wg[i], ffn2_wu[i], ffn2_wd[i])
        xp = xp + jax.nn.sigmoid(rmsnorm(xp, norm_ple[i]) @ ple_w_gate[i]) * (p_prompt[i].astype(xp.dtype) @ ple_w_proj[i])
        xs = xs + jax.nn.sigmoid(rmsnorm(xs, norm_ple[i]) @ ple_w_gate[i]) * (p_sample[i].astype(xs.dtype) @ ple_w_proj[i])
    new_k_prompt = jnp.stack(kp_l)
    new_v_prompt = jnp.stack(vp_l)
    new_k_sample = jnp.stack(ks_l)
    new_v_sample = jnp.stack(vs_l)
    new_conv_prompt = jnp.stack(cp_l)
    new_conv_sample = jnp.stack(cs_l)
    return (xp, xs, new_k_prompt, new_v_prompt, new_k_sample, new_v_sample, new_conv_prompt, new_conv_sample)
```

```python
import functools
import math

import jax
import jax.numpy as jnp
from jax import lax
from jax.experimental import pallas as pl
from jax.experimental.pallas import tpu as pltpu

F32 = jnp.float32
BF16 = jnp.bfloat16

D_MODEL = 1024
N_HEADS = 16
HEAD_DIM = D_MODEL // N_HEADS
D_FF = 2816
CONV_WIDTH = 31
D_PLE = 256
PAGE_SIZE = 128
RMS_EPS = 1e-6
LN_EPS = 1e-5

LANES = 128
SUBLANES = 8
MXU_DIM = 256
VMEM_LIMIT_BYTES = 56 * 1024 * 1024

FF_CHUNKS = ((0, 6 * MXU_DIM), (6 * MXU_DIM, D_FF))
ATTN_BLOCK = 256
CONV_ROWS = 32
CONV_HALO = 32


def _const_spec(shape):
    nd = len(shape)
    return pl.BlockSpec(shape, lambda *_: (0,) * nd, pipeline_mode=pl.Buffered(1))


def _rms(x, g):
    ms = jnp.mean(x * x, axis=-1, keepdims=True)
    return x * lax.rsqrt(ms + RMS_EPS) * g


def _dot(a, b):
    return jnp.dot(a, b, preferred_element_type=F32)


def _softplus(z):
    return jnp.maximum(z, 0.0) + jnp.log(1.0 + jnp.exp(-jnp.abs(z)))


def _split_bf16(x):
    hi = x.astype(BF16)
    lo = (x - hi.astype(F32)).astype(BF16)
    return hi, lo


def _block_kernel(*refs, has_pro, has_epi):
    it = iter(refs)
    x_ref = next(it)
    if has_pro:
        o_ref, wpro_ref = next(it), next(it)
    gf_ref, wg_ref, wu_ref, wd_ref = next(it), next(it), next(it), next(it)
    if has_epi:
        p_ref, gp_ref, wgate_ref, wproj_ref = next(it), next(it), next(it), next(it)
    out_ref = next(it)

    x = x_ref[...]
    if has_pro:
        x = x + _dot(o_ref[...], wpro_ref[...])
    h = _rms(x, gf_ref[...]).astype(BF16)
    acc = None
    for s, e in FF_CHUNKS:
        hg = _dot(h, wg_ref[:, s:e])
        hu = _dot(h, wu_ref[:, s:e])
        a = (hg * jax.nn.sigmoid(hg) * hu).astype(BF16)
        d = _dot(a, wd_ref[s:e, :])
        acc = d if acc is None else acc + d
    x = x + 0.5 * acc
    if has_epi:
        hp = _rms(x, gp_ref[...]).astype(BF16)
        gate = jax.nn.sigmoid(_dot(hp, wgate_ref[...]))
        x = x + gate * _dot(p_ref[...].astype(BF16), wproj_ref[...])
    out_ref[...] = x


def _block(x, ffn, pro=None, epi=None, *, tm):
    n = x.shape[0]
    row = lambda w: pl.BlockSpec((tm, w), lambda i: (i, 0))
    args, specs = [x], [row(D_MODEL)]
    if pro is not None:
        o, w = pro
        args += [o, w]
        specs += [row(D_MODEL), _const_spec(w.shape)]
    g, wg, wu, wd = ffn
    args += [g, wg, wu, wd]
    specs += [_const_spec(a.shape) for a in (g, wg, wu, wd)]
    if epi is not None:
        p, gp, wgate, wproj = epi
        args += [p, gp, wgate, wproj]
        specs += [row(D_PLE)] + [_const_spec(a.shape) for a in (gp, wgate, wproj)]
    return pl.pallas_call(
        functools.partial(_block_kernel, has_pro=pro is not None, has_epi=epi is not None),
        grid=(n // tm,),
        in_specs=specs,
        out_specs=row(D_MODEL),
        out_shape=jax.ShapeDtypeStruct((n, D_MODEL), F32),
        compiler_params=pltpu.CompilerParams(
            dimension_semantics=("arbitrary",), vmem_limit_bytes=VMEM_LIMIT_BYTES),
        name="ffn_block",
    )(*args)


def _qkv_kernel(x_ref, g_ref, w_ref, qg_ref, kg_ref,
                q_ref, k_ref, v_ref, kb_ref, vb_ref):
    h = _rms(x_ref[...], g_ref[...]).astype(BF16)
    qkv = _dot(h, w_ref[...])
    r = lax.broadcasted_iota(jnp.int32, (MXU_DIM, MXU_DIM), 0) // HEAD_DIM
    c = lax.broadcasted_iota(jnp.int32, (MXU_DIM, MXU_DIM), 1) // HEAD_DIM
    avg = jnp.where(r == c, 1.0 / HEAD_DIM, 0.0).astype(BF16)
    scale = 1.0 / math.sqrt(HEAD_DIM)
    for j in range(D_MODEL // MXU_DIM):
        sl = slice(j * MXU_DIM, (j + 1) * MXU_DIM)
        q = qkv[:, sl]
        ms = _dot((q * q).astype(BF16), avg)
        qn = q * lax.rsqrt(ms + RMS_EPS) * qg_ref[:, sl]
        q_ref[:, sl] = (qn * scale).astype(BF16)
        k = qkv[:, D_MODEL + j * MXU_DIM:D_MODEL + (j + 1) * MXU_DIM]
        ms = _dot((k * k).astype(BF16), avg)
        kn = k * lax.rsqrt(ms + RMS_EPS) * kg_ref[:, sl]
        k_ref[:, sl] = kn
        kb_ref[:, sl] = kn.astype(BF16)
    v = qkv[:, 2 * D_MODEL:]
    v_ref[...] = v
    vb_ref[...] = v.astype(BF16)


def _qkv(x, g, w, qg, kg, *, tm):
    n = x.shape[0]
    row = pl.BlockSpec((tm, D_MODEL), lambda i: (i, 0))
    sds = lambda dt: jax.ShapeDtypeStruct((n, D_MODEL), dt)
    return pl.pallas_call(
        _qkv_kernel,
        grid=(n // tm,),
        in_specs=[row] + [_const_spec(a.shape) for a in (g, w, qg, kg)],
        out_specs=[row] * 5,
        out_shape=[sds(BF16), sds(F32), sds(F32), sds(BF16), sds(BF16)],
        compiler_params=pltpu.CompilerParams(
            dimension_semantics=("arbitrary",), vmem_limit_bytes=VMEM_LIMIT_BYTES),
        name="qkv_proj",
    )(x, g, w, qg, kg)


def _attn_kernel(bias_ref, q_ref, k_ref, v_ref, o_ref):
    t = ATTN_BLOCK
    hp = pl.program_id(1)
    i = pl.program_id(2)
    biases = (bias_ref[2 * hp], bias_ref[2 * hp + 1])

    first_head = lax.broadcasted_iota(jnp.int32, (1, LANES), 1) < HEAD_DIM
    q2 = q_ref[...]
    qz = jnp.zeros_like(q2)
    qs = jnp.concatenate([jnp.where(first_head, q2, qz), jnp.where(first_head, qz, q2)], axis=0)

    row = lax.broadcasted_iota(jnp.int32, (t, t), 0)
    col = lax.broadcasted_iota(jnp.int32, (t, t), 1)
    suffix = jnp.where(row >= col, 1.0, 0.0).astype(BF16)
    causal = col < row

    def step(j, carry, masked):
        runs, acc = carry[:2], carry[2]
        start = pl.multiple_of(j * t, t)
        kj = k_ref[pl.ds(start, t), :]
        vj = v_ref[pl.ds(start, t), :]
        z = lax.dot_general(qs, kj, (((1,), (1,)), ((), ())), preferred_element_type=F32)
        probs, new_runs = [], []
        for hh in range(2):
            zh = z[hh * t:(hh + 1) * t] + biases[hh]
            sp = _softplus(zh)
            if masked:
                sp = jnp.where(causal, sp, 0.0)
            hi, lo = _split_bf16(sp)
            ssum = _dot(hi, suffix) + _dot(lo, suffix)
            a = jnp.exp(zh - ssum - runs[hh])
            if masked:
                a = jnp.where(causal, a, 0.0)
            probs.append(a.astype(BF16))
            new_runs.append(runs[hh] + ssum[:, 0:1])
        a2 = jnp.concatenate(probs, axis=1)
        vz = jnp.zeros_like(vj)
        vs = jnp.concatenate([jnp.where(first_head, vj, vz), jnp.where(first_head, vz, vj)], axis=0)
        return new_runs[0], new_runs[1], acc + _dot(a2, vs)

    zero_run = jnp.zeros((t, 1), F32)
    carry = step(i, (zero_run, zero_run, jnp.zeros((t, LANES), F32)), True)
    carry = lax.fori_loop(0, i, lambda jj, c: step(i - 1 - jj, c, False), carry)
    o_ref[...] = carry[2].astype(o_ref.dtype)


def _attn_prompt(q, kb, vb, bias, *, batch, seq):
    t = ATTN_BLOCK
    nq = seq // t
    qo_spec = pl.BlockSpec((t, LANES), lambda b, hp, i: (b * nq + i, hp))
    kv_spec = pl.BlockSpec((seq, LANES), lambda b, hp, i: (b, hp))
    return pl.pallas_call(
        _attn_kernel,
        grid=(batch, N_HEADS // 2, nq),
        in_specs=[pl.BlockSpec(memory_space=pltpu.SMEM), qo_spec, kv_spec, kv_spec],
        out_specs=qo_spec,
        out_shape=jax.ShapeDtypeStruct(q.shape, BF16),
        compiler_params=pltpu.CompilerParams(
            dimension_semantics=("arbitrary", "arbitrary", "arbitrary"),
            vmem_limit_bytes=VMEM_LIMIT_BYTES),
        name="sb_attn_prompt",
    )(bias, q, kb, vb)


def _decode_kernel(pt_ref, q_ref, bias_ref, k_ref, v_ref, o_ref, qbd_ref, run_ref, acc_ref,
                   *, n_pages):
    del pt_ref
    pi = pl.program_id(1)

    @pl.when(pi == 0)
    def _():
        r = lax.broadcasted_iota(jnp.int32, (HEAD_DIM, HEAD_DIM), 0)
        c = lax.broadcasted_iota(jnp.int32, (HEAD_DIM, HEAD_DIM), 1)
        ones = jnp.ones((HEAD_DIM, PAGE_SIZE), BF16)
        for h in range(N_HEADS):
            qh = jnp.broadcast_to(q_ref[h:h + 1, :].astype(F32), (HEAD_DIM, HEAD_DIM))
            qbd_ref[h] = _dot(jnp.where(r == c, qh, 0.0).astype(BF16), ones)
        run_ref[...] = jnp.zeros_like(run_ref)
        acc_ref[...] = jnp.zeros_like(acc_ref)

    z = jnp.concatenate(
        [jnp.sum(k_ref[h] * qbd_ref[h], axis=0, keepdims=True) for h in range(N_HEADS)],
        axis=0) + bias_ref[...]
    sp = _softplus(z)
    r = lax.broadcasted_iota(jnp.int32, (PAGE_SIZE, PAGE_SIZE), 0)
    c = lax.broadcasted_iota(jnp.int32, (PAGE_SIZE, PAGE_SIZE), 1)
    suffix = jnp.where(r >= c, 1.0, 0.0).astype(BF16)
    hi, lo = _split_bf16(sp)
    ssum = _dot(hi, suffix) + _dot(lo, suffix)
    run = run_ref[...]
    a = jnp.exp(z - ssum - run[:, 0:1])
    run_ref[...] = run + ssum[:, 0:1]
    for h in range(N_HEADS):
        acc_ref[h] += a[h:h + 1, :] * v_ref[h]

    @pl.when(pi == n_pages - 1)
    def _():
        ones = jnp.ones((SUBLANES, PAGE_SIZE), BF16)
        nt = (((1,), (1,)), ((), ()))
        for h in range(N_HEADS):
            hi_h, lo_h = _split_bf16(acc_ref[h])
            tot = (lax.dot_general(ones, hi_h, nt, preferred_element_type=F32)
                   + lax.dot_general(ones, lo_h, nt, preferred_element_type=F32))
            o_ref[h:h + 1, :] = tot[0:1, :].astype(o_ref.dtype)


def _attn_decode(q, cache_kt, cache_vt, page_table, bias):
    n_seq, n_pages = page_table.shape
    page_spec = pl.BlockSpec((None, N_HEADS, HEAD_DIM, PAGE_SIZE),
                             lambda b, pi, pt: (pt[b * n_pages + n_pages - 1 - pi], 0, 0, 0))
    head_spec = pl.BlockSpec((None, N_HEADS, HEAD_DIM), lambda b, pi, pt: (b, 0, 0))
    out = pl.pallas_call(
        functools.partial(_decode_kernel, n_pages=n_pages),
        grid_spec=pltpu.PrefetchScalarGridSpec(
            num_scalar_prefetch=1,
            grid=(n_seq, n_pages),
            in_specs=[head_spec,
                      pl.BlockSpec((N_HEADS, 1), lambda b, pi, pt: (0, 0)),
                      page_spec, page_spec],
            out_specs=head_spec,
            scratch_shapes=[pltpu.VMEM((N_HEADS, HEAD_DIM, PAGE_SIZE), F32),
                            pltpu.VMEM((N_HEADS, PAGE_SIZE), F32),
                            pltpu.VMEM((N_HEADS, HEAD_DIM, PAGE_SIZE), F32)],
        ),
        out_shape=jax.ShapeDtypeStruct((n_seq, N_HEADS, HEAD_DIM), BF16),
        compiler_params=pltpu.CompilerParams(
            dimension_semantics=("arbitrary", "arbitrary"), vmem_limit_bytes=VMEM_LIMIT_BYTES),
        name="sb_attn_decode",
    )(page_table.reshape(-1), q.reshape(n_seq, N_HEADS, HEAD_DIM), bias.reshape(N_HEADS, 1),
      cache_kt, cache_vt)
    return out.reshape(n_seq, D_MODEL)


def _glu_kernel(x_ref, g_ref, w_ref, out_ref):
    h = _rms(x_ref[...], g_ref[...]).astype(BF16)
    u = _dot(h, w_ref[...])
    out_ref[...] = u[:, :D_MODEL] * jax.nn.sigmoid(u[:, D_MODEL:])


def _glu(x, g, w, *, tm):
    n = x.shape[0]
    row = pl.BlockSpec((tm, D_MODEL), lambda i: (i, 0))
    return pl.pallas_call(
        _glu_kernel,
        grid=(n // tm,),
        in_specs=[row, _const_spec(g.shape), _const_spec(w.shape)],
        out_specs=row,
        out_shape=jax.ShapeDtypeStruct((n, D_MODEL), F32),
        compiler_params=pltpu.CompilerParams(
            dimension_semantics=("arbitrary",), vmem_limit_bytes=VMEM_LIMIT_BYTES),
        name="conv_glu",
    )(x, g, w)


def _ln_silu(y, g, b):
    mu = jnp.mean(y, axis=-1, keepdims=True)
    yc = y - mu
    var = jnp.mean(yc * yc, axis=-1, keepdims=True)
    yn = yc * lax.rsqrt(var + LN_EPS) * g + b
    return yn * jax.nn.sigmoid(yn)


def _conv_prompt_kernel(cur_ref, prev_ref, w_ref, b_ref, lng_ref, lnb_ref, out_ref, ext_ref,
                        *, tm, tiles_per_seq):
    i = pl.program_id(0)
    keep = jnp.where(i % tiles_per_seq == 0, 0.0, 1.0)
    ext_ref[0:CONV_HALO, :] = prev_ref[...] * keep
    ext_ref[CONV_HALO:CONV_HALO + tm, :] = cur_ref[...]
    first_tap = CONV_HALO - (CONV_WIDTH - 1)

    win_rows = CONV_ROWS + CONV_HALO

    def chunk(c, carry):
        r0 = pl.multiple_of(c * CONV_ROWS, CONV_ROWS)
        win = ext_ref[pl.ds(r0, win_rows), :]
        acc = jnp.zeros((CONV_ROWS, D_MODEL), F32)
        for b in range(SUBLANES):
            shifted = pltpu.roll(win, (win_rows - first_tap - b) % win_rows, 0)
            for a in range(-(-CONV_WIDTH // SUBLANES)):
                k = SUBLANES * a + b
                if k < CONV_WIDTH:
                    acc = acc + w_ref[k:k + 1, :] * shifted[SUBLANES * a:SUBLANES * a + CONV_ROWS]
        y = _ln_silu(acc + b_ref[...], lng_ref[...], lnb_ref[...])
        out_ref[pl.ds(r0, CONV_ROWS), :] = y.astype(out_ref.dtype)
        return carry

    lax.fori_loop(0, tm // CONV_ROWS, chunk, 0)


def _conv_prompt(g, w, b, lng, lnb, *, tm, seq):
    n = g.shape[0]
    halo_per_tile = tm // CONV_HALO
    cur = pl.BlockSpec((tm, D_MODEL), lambda i: (i, 0))
    prev = pl.BlockSpec((CONV_HALO, D_MODEL), lambda i: (jnp.maximum(i * halo_per_tile - 1, 0), 0))
    return pl.pallas_call(
        functools.partial(_conv_prompt_kernel, tm=tm, tiles_per_seq=seq // tm),
        grid=(n // tm,),
        in_specs=[cur, prev] + [_const_spec(a.shape) for a in (w, b, lng, lnb)],
        out_specs=cur,
        out_shape=jax.ShapeDtypeStruct((n, D_MODEL), BF16),
        scratch_shapes=[pltpu.VMEM((CONV_HALO + tm, D_MODEL), F32)],
        compiler_params=pltpu.CompilerParams(
            dimension_semantics=("arbitrary",), vmem_limit_bytes=VMEM_LIMIT_BYTES),
        name="conv_dw_prompt",
    )(g, g, w, b, lng, lnb)


def _conv_sample_kernel(state_ref, g_ref, w_ref, b_ref, lng_ref, lnb_ref, out_ref):
    acc = w_ref[CONV_WIDTH - 1:CONV_WIDTH, :] * g_ref[...]
    for k in range(CONV_WIDTH - 1):
        acc = acc + w_ref[k:k + 1, :] * state_ref[k]
    out_ref[...] = _ln_silu(acc + b_ref[...], lng_ref[...], lnb_ref[...]).astype(out_ref.dtype)


def _conv_sample(state, g, w, b, lng, lnb, *, tb):
    n = g.shape[0]
    row = pl.BlockSpec((tb, D_MODEL), lambda i: (i, 0))
    return pl.pallas_call(
        _conv_sample_kernel,
        grid=(n // tb,),
        in_specs=[pl.BlockSpec((CONV_WIDTH - 1, tb, D_MODEL), lambda i: (0, i, 0)), row]
        + [_const_spec(a.shape) for a in (w, b, lng, lnb)],
        out_specs=row,
        out_shape=jax.ShapeDtypeStruct((n, D_MODEL), BF16),
        compiler_params=pltpu.CompilerParams(
            dimension_semantics=("arbitrary",), vmem_limit_bytes=VMEM_LIMIT_BYTES),
        name="conv_dw_sample",
    )(state, g, w, b, lng, lnb)


def kernel(x_prompt, x_sample, cache_k, cache_v, state_conv, page_table, p_prompt, p_sample, norm_ffn1, ffn1_wg, ffn1_wu, ffn1_wd, norm_mix, attn_w_qkv, attn_q_norm, attn_k_norm, attn_logit_bias, attn_w_o, conv_w_in, conv_w_dw, conv_b_dw, conv_ln_g, conv_ln_b, conv_w_out, norm_ffn2, ffn2_wg, ffn2_wu, ffn2_wd, norm_ple, ple_w_gate, ple_w_proj):
    bp, tp, _ = x_prompt.shape
    bs, ts, _ = x_sample.shape
    assert ts == 1
    n_phys = cache_k.shape[1]
    tm_p, tm_s = 512, bs

    vec = lambda a: a.reshape(1, -1)
    bf = lambda a: a.astype(BF16)
    ffn1 = [(vec(norm_ffn1[i]), bf(ffn1_wg[i]), bf(ffn1_wu[i]), bf(ffn1_wd[i])) for i in range(2)]
    ffn2 = [(vec(norm_ffn2[i]), bf(ffn2_wg[i]), bf(ffn2_wu[i]), bf(ffn2_wd[i])) for i in range(2)]
    ple_w = [(vec(norm_ple[i]), bf(ple_w_gate[i]), bf(ple_w_proj[i])) for i in range(2)]
    w_qkv, w_o = bf(attn_w_qkv[0]), bf(attn_w_o[0])
    qg = jnp.tile(attn_q_norm[0], N_HEADS).reshape(1, D_MODEL)
    kg = jnp.tile(attn_k_norm[0], N_HEADS).reshape(1, D_MODEL)
    bias = attn_logit_bias[0]
    w_in, w_out = bf(conv_w_in[0]), bf(conv_w_out[0])
    w_dw, b_dw = conv_w_dw[0], vec(conv_b_dw[0])
    ln_g, ln_b = vec(conv_ln_g[0]), vec(conv_ln_b[0])

    xp = x_prompt.reshape(bp * tp, D_MODEL)
    xs = x_sample.reshape(bs, D_MODEL)
    pp = p_prompt.reshape(2, bp * tp, D_PLE)
    ps = p_sample.reshape(2, bs, D_PLE)

    xp = _block(xp, ffn1[0], tm=tm_p)
    xs = _block(xs, ffn1[0], tm=tm_s)
    qp, kp, vp, kpb, vpb = _qkv(xp, vec(norm_mix[0]), w_qkv, qg, kg, tm=tm_p)
    qs, ks, vs, _, _ = _qkv(xs, vec(norm_mix[0]), w_qkv, qg, kg, tm=tm_s)
    op = _attn_prompt(qp, kpb, vpb, bias, batch=bp, seq=tp)
    os_ = _attn_decode(qs, jnp.transpose(cache_k[0], (0, 2, 3, 1)),
                       jnp.transpose(cache_v[0], (0, 2, 3, 1)), page_table, bias)
    xp = _block(xp, ffn2[0], pro=(op, w_o), epi=(pp[0],) + ple_w[0], tm=tm_p)
    xs = _block(xs, ffn2[0], pro=(os_, w_o), epi=(ps[0],) + ple_w[0], tm=tm_s)

    xp = _block(xp, ffn1[1], tm=tm_p)
    xs = _block(xs, ffn1[1], tm=tm_s)
    gp = _glu(xp, vec(norm_mix[1]), w_in, tm=tm_p)
    gs = _glu(xs, vec(norm_mix[1]), w_in, tm=tm_s)
    cp = _conv_prompt(gp, w_dw, b_dw, ln_g, ln_b, tm=tm_p, seq=tp)
    state_t = jnp.transpose(state_conv[0], (1, 0, 2))
    cs = _conv_sample(state_t, gs, w_dw, b_dw, ln_g, ln_b, tb=32)
    xp = _block(xp, ffn2[1], pro=(cp, w_out), epi=(pp[1],) + ple_w[1], tm=tm_p)
    xs = _block(xs, ffn2[1], pro=(cs, w_out), epi=(ps[1],) + ple_w[1], tm=tm_s)

    hist = CONV_WIDTH - 1
    new_conv_prompt = gp.reshape(bp, tp, D_MODEL)[:, tp - hist:]
    new_conv_sample = jnp.transpose(jnp.concatenate([state_t[1:], gs[None]], axis=0), (1, 0, 2))
    return (xp.reshape(bp, tp, D_MODEL), xs.reshape(bs, ts, D_MODEL),
            kp.reshape(1, bp, tp, N_HEADS, HEAD_DIM), vp.reshape(1, bp, tp, N_HEADS, HEAD_DIM),
            ks.reshape(1, bs, ts, N_HEADS, HEAD_DIM), vs.reshape(1, bs, ts, N_HEADS, HEAD_DIM),
            new_conv_prompt[None], new_conv_sample[None])
```

```python
import functools
import math

import jax
import jax.numpy as jnp
from jax import lax
from jax.experimental import pallas as pl
from jax.experimental.pallas import tpu as pltpu

F32 = jnp.float32
BF16 = jnp.bfloat16

D_MODEL = 1024
N_HEADS = 16
HEAD_DIM = D_MODEL // N_HEADS
D_FF = 2816
CONV_WIDTH = 31
D_PLE = 256
PAGE_SIZE = 128
RMS_EPS = 1e-6
LN_EPS = 1e-5

LANES = 128
SUBLANES = 8
MXU_DIM = 256
VMEM_LIMIT_BYTES = 56 * 1024 * 1024

FF_CHUNKS = ((0, 6 * MXU_DIM), (6 * MXU_DIM, D_FF))
ATTN_BLOCK = 256
ATTN_HEADS = MXU_DIM // HEAD_DIM
LOG2E = 1.4426950408889634
CONV_ROWS = 32
CONV_HALO = 32


def _const_spec(shape):
    nd = len(shape)
    return pl.BlockSpec(shape, lambda *_: (0,) * nd, pipeline_mode=pl.Buffered(1))


def _rms(x, g):
    ms = jnp.mean(x * x, axis=-1, keepdims=True)
    return x * lax.rsqrt(ms + RMS_EPS) * g


def _dot(a, b):
    return jnp.dot(a, b, preferred_element_type=F32)


def _softplus2(z2):
    return jnp.maximum(z2, 0.0) + jnp.log2(1.0 + jnp.exp2(-jnp.abs(z2)))


def _strictly_after(n):
    r = lax.broadcasted_iota(jnp.int32, (n, n), 0)
    c = lax.broadcasted_iota(jnp.int32, (n, n), 1)
    return jnp.where(r > c, 1.0, 0.0).astype(BF16)


def _split_bf16(x):
    hi = x.astype(BF16)
    lo = (x - hi.astype(F32)).astype(BF16)
    return hi, lo


def _block_kernel(*refs, has_pro, has_epi):
    it = iter(refs)
    x_ref = next(it)
    if has_pro:
        o_ref, wpro_ref = next(it), next(it)
    gf_ref, wg_ref, wu_ref, wd_ref = next(it), next(it), next(it), next(it)
    if has_epi:
        p_ref, gp_ref, wgate_ref, wproj_ref = next(it), next(it), next(it), next(it)
    out_ref = next(it)

    x = x_ref[...]
    if has_pro:
        x = x + _dot(o_ref[...], wpro_ref[...])
    h = _rms(x, gf_ref[...]).astype(BF16)
    acc = None
    for s, e in FF_CHUNKS:
        hg = _dot(h, wg_ref[:, s:e])
        hu = _dot(h, wu_ref[:, s:e])
        a = (hg * jax.nn.sigmoid(hg) * hu).astype(BF16)
        d = _dot(a, wd_ref[s:e, :])
        acc = d if acc is None else acc + d
    x = x + 0.5 * acc
    if has_epi:
        hp = _rms(x, gp_ref[...]).astype(BF16)
        gate = jax.nn.sigmoid(_dot(hp, wgate_ref[...]))
        x = x + gate * _dot(p_ref[...].astype(BF16), wproj_ref[...])
    out_ref[...] = x


def _block(x, ffn, pro=None, epi=None, *, tm):
    n = x.shape[0]
    row = lambda w: pl.BlockSpec((tm, w), lambda i: (i, 0))
    args, specs = [x], [row(D_MODEL)]
    if pro is not None:
        o, w = pro
        args += [o, w]
        specs += [row(D_MODEL), _const_spec(w.shape)]
    g, wg, wu, wd = ffn
    args += [g, wg, wu, wd]
    specs += [_const_spec(a.shape) for a in (g, wg, wu, wd)]
    if epi is not None:
        p, gp, wgate, wproj = epi
        args += [p, gp, wgate, wproj]
        specs += [row(D_PLE)] + [_const_spec(a.shape) for a in (gp, wgate, wproj)]
    return pl.pallas_call(
        functools.partial(_block_kernel, has_pro=pro is not None, has_epi=epi is not None),
        grid=(n // tm,),
        in_specs=specs,
        out_specs=row(D_MODEL),
        out_shape=jax.ShapeDtypeStruct((n, D_MODEL), F32),
        compiler_params=pltpu.CompilerParams(
            dimension_semantics=("arbitrary",), vmem_limit_bytes=VMEM_LIMIT_BYTES),
        name="ffn_block",
    )(*args)


def _qkv_kernel(x_ref, g_ref, w_ref, qg_ref, kg_ref,
                q_ref, k_ref, v_ref, kb_ref, vb_ref):
    h = _rms(x_ref[...], g_ref[...]).astype(BF16)
    qkv = _dot(h, w_ref[...])
    r = lax.broadcasted_iota(jnp.int32, (MXU_DIM, MXU_DIM), 0) // HEAD_DIM
    c = lax.broadcasted_iota(jnp.int32, (MXU_DIM, MXU_DIM), 1) // HEAD_DIM
    avg = jnp.where(r == c, 1.0 / HEAD_DIM, 0.0).astype(BF16)
    scale = LOG2E / math.sqrt(HEAD_DIM)
    for j in range(D_MODEL // MXU_DIM):
        sl = slice(j * MXU_DIM, (j + 1) * MXU_DIM)
        q = qkv[:, sl]
        ms = _dot((q * q).astype(BF16), avg)
        qn = q * lax.rsqrt(ms + RMS_EPS) * qg_ref[:, sl]
        q_ref[:, sl] = (qn * scale).astype(BF16)
        k = qkv[:, D_MODEL + j * MXU_DIM:D_MODEL + (j + 1) * MXU_DIM]
        ms = _dot((k * k).astype(BF16), avg)
        kn = k * lax.rsqrt(ms + RMS_EPS) * kg_ref[:, sl]
        k_ref[:, sl] = kn
        kb_ref[:, sl] = kn.astype(BF16)
    v = qkv[:, 2 * D_MODEL:]
    v_ref[...] = v
    vb_ref[...] = v.astype(BF16)


def _qkv(x, g, w, qg, kg, *, tm):
    n = x.shape[0]
    row = pl.BlockSpec((tm, D_MODEL), lambda i: (i, 0))
    sds = lambda dt: jax.ShapeDtypeStruct((n, D_MODEL), dt)
    return pl.pallas_call(
        _qkv_kernel,
        grid=(n // tm,),
        in_specs=[row] + [_const_spec(a.shape) for a in (g, w, qg, kg)],
        out_specs=[row] * 5,
        out_shape=[sds(BF16), sds(F32), sds(F32), sds(BF16), sds(BF16)],
        compiler_params=pltpu.CompilerParams(
            dimension_semantics=("arbitrary",), vmem_limit_bytes=VMEM_LIMIT_BYTES),
        name="qkv_proj",
    )(x, g, w, qg, kg)


def _attn_kernel(bias_ref, q_ref, k_ref, v_ref, o_ref, z_ref, a_ref):
    t = ATTN_BLOCK
    nh = ATTN_HEADS
    hg = pl.program_id(1)
    i = pl.program_id(2)
    biases = [bias_ref[nh * hg + h] for h in range(nh)]

    head_of_lane = lax.broadcasted_iota(jnp.int32, (1, nh * HEAD_DIM), 1) // HEAD_DIM
    q4 = q_ref[...]
    qs = jnp.concatenate(
        [jnp.where(head_of_lane == h, q4, jnp.zeros_like(q4)) for h in range(nh)], axis=0)

    after_mat = _strictly_after(t)
    row = lax.broadcasted_iota(jnp.int32, (t, t), 0)
    col = lax.broadcasted_iota(jnp.int32, (t, t), 1)
    causal = col < row

    def scores(j):
        kj = k_ref[pl.ds(pl.multiple_of(j * t, t), t), :]
        return lax.dot_general(qs, kj, (((1,), (1,)), ((), ())), preferred_element_type=F32)

    def weighted_values(a4, j):
        vj = v_ref[pl.ds(pl.multiple_of(j * t, t), t), :]
        vs = jnp.concatenate(
            [jnp.where(head_of_lane == h, vj, jnp.zeros_like(vj)) for h in range(nh)], axis=0)
        return _dot(a4, vs)

    def weights(z, runs, masked):
        probs, new_runs = [], []
        for h in range(nh):
            zh = z[h * t:(h + 1) * t] + biases[h]
            sp = _softplus2(zh)
            if masked:
                sp = jnp.where(causal, sp, 0.0)
            after = _dot(sp.astype(BF16), after_mat)
            a = jnp.exp2(zh - sp - after - runs[h])
            if masked:
                a = jnp.where(causal, a, 0.0)
            probs.append(a.astype(BF16))
            new_runs.append(runs[h] + after[:, 0:1] + sp[:, 0:1])
        return jnp.concatenate(probs, axis=1), tuple(new_runs)

    a4, runs = weights(scores(i), (jnp.zeros((t, 1), F32),) * nh, True)
    a_ref[...] = a4
    z_ref[...] = scores(jnp.maximum(i - 1, 0))

    def body(jj, carry):
        runs, acc = carry[:nh], carry[nh]
        j = i - 1 - jj
        acc = acc + weighted_values(a_ref[...], j + 1)
        a4, runs = weights(z_ref[...], runs, False)
        z_ref[...] = scores(jnp.maximum(j - 1, 0))
        a_ref[...] = a4
        return runs + (acc,)

    carry = lax.fori_loop(0, i, body, runs + (jnp.zeros((t, nh * HEAD_DIM), F32),))
    o_ref[...] = (carry[nh] + weighted_values(a_ref[...], 0)).astype(o_ref.dtype)


def _attn_prompt(q, kb, vb, bias, *, batch, seq):
    t = ATTN_BLOCK
    nq = seq // t
    width = ATTN_HEADS * HEAD_DIM
    qo_spec = pl.BlockSpec((t, width), lambda b, hp, i: (b * nq + i, hp))
    kv_spec = pl.BlockSpec((seq, width), lambda b, hp, i: (b, hp))
    return pl.pallas_call(
        _attn_kernel,
        grid=(batch, N_HEADS // ATTN_HEADS, nq),
        in_specs=[pl.BlockSpec(memory_space=pltpu.SMEM), qo_spec, kv_spec, kv_spec],
        out_specs=qo_spec,
        out_shape=jax.ShapeDtypeStruct(q.shape, BF16),
        scratch_shapes=[pltpu.VMEM((ATTN_HEADS * t, t), F32),
                        pltpu.VMEM((t, ATTN_HEADS * t), BF16)],
        compiler_params=pltpu.CompilerParams(
            dimension_semantics=("arbitrary", "arbitrary", "arbitrary"),
            vmem_limit_bytes=VMEM_LIMIT_BYTES),
        name="sb_attn_prompt",
    )(bias, q, kb, vb)


def _decode_kernel(pt_ref, q_ref, bias_ref, *refs, n_pages):
    del pt_ref
    k_refs, v_refs = refs[:n_pages], refs[n_pages:2 * n_pages]
    o_ref, qb_ref = refs[2 * n_pages], refs[2 * n_pages + 1]

    r = lax.broadcasted_iota(jnp.int32, (HEAD_DIM, HEAD_DIM), 0)
    c = lax.broadcasted_iota(jnp.int32, (HEAD_DIM, HEAD_DIM), 1)
    ones = jnp.ones((HEAD_DIM, PAGE_SIZE), BF16)
    for h in range(N_HEADS):
        qh = jnp.broadcast_to(q_ref[h:h + 1, :].astype(F32), (HEAD_DIM, HEAD_DIM))
        qb_ref[h] = _dot(jnp.where(r == c, qh, 0.0).astype(BF16), ones)

    z = jnp.concatenate(
        [jnp.sum(k_refs[p][h] * qb_ref[h], axis=0, keepdims=True)
         for p in range(n_pages) for h in range(N_HEADS)], axis=0) + bias_ref[...]
    sp = _softplus2(z)
    after_mat = _strictly_after(PAGE_SIZE)
    hi, lo = _split_bf16(sp)
    after = _dot(hi, after_mat) + _dot(lo, after_mat)
    total = after[:, 0:1] + sp[:, 0:1]
    run = jnp.zeros((N_HEADS, 1), F32)
    probs = [None] * n_pages
    for p in reversed(range(n_pages)):
        rows = slice(p * N_HEADS, (p + 1) * N_HEADS)
        probs[p] = jnp.exp2(z[rows] - sp[rows] - after[rows] - run)
        run = run + total[rows]

    ones8 = jnp.ones((SUBLANES, PAGE_SIZE), BF16)
    nt = (((1,), (1,)), ((), ()))
    for h in range(N_HEADS):
        acc = probs[0][h:h + 1, :] * v_refs[0][h]
        for p in range(1, n_pages):
            acc = acc + probs[p][h:h + 1, :] * v_refs[p][h]
        hi_h, lo_h = _split_bf16(acc)
        tot = (lax.dot_general(ones8, hi_h, nt, preferred_element_type=F32)
               + lax.dot_general(ones8, lo_h, nt, preferred_element_type=F32))
        o_ref[h:h + 1, :] = tot[0:1, :].astype(o_ref.dtype)


def _attn_decode(q, cache_kt, cache_vt, page_table, bias):
    n_seq, n_pages = page_table.shape

    def page_spec(p):
        return pl.BlockSpec((None, N_HEADS, HEAD_DIM, PAGE_SIZE),
                            lambda b, pt: (pt[b * n_pages + p], 0, 0, 0))

    head_spec = pl.BlockSpec((None, N_HEADS, HEAD_DIM), lambda b, pt: (b, 0, 0))
    pages = [page_spec(p) for p in range(n_pages)]
    out = pl.pallas_call(
        functools.partial(_decode_kernel, n_pages=n_pages),
        grid_spec=pltpu.PrefetchScalarGridSpec(
            num_scalar_prefetch=1,
            grid=(n_seq,),
            in_specs=[head_spec,
                      pl.BlockSpec((n_pages * N_HEADS, 1), lambda b, pt: (0, 0))] + pages + pages,
            out_specs=head_spec,
            scratch_shapes=[pltpu.VMEM((N_HEADS, HEAD_DIM, PAGE_SIZE), F32)],
        ),
        out_shape=jax.ShapeDtypeStruct((n_seq, N_HEADS, HEAD_DIM), BF16),
        compiler_params=pltpu.CompilerParams(
            dimension_semantics=("arbitrary",), vmem_limit_bytes=VMEM_LIMIT_BYTES),
        name="sb_attn_decode",
    )(page_table.reshape(-1), q.reshape(n_seq, N_HEADS, HEAD_DIM),
      jnp.tile(bias.reshape(N_HEADS, 1), (n_pages, 1)),
      *([cache_kt] * n_pages), *([cache_vt] * n_pages))
    return out.reshape(n_seq, D_MODEL)


def _glu_kernel(x_ref, g_ref, w_ref, out_ref):
    h = _rms(x_ref[...], g_ref[...]).astype(BF16)
    u = _dot(h, w_ref[...])
    out_ref[...] = u[:, :D_MODEL] * jax.nn.sigmoid(u[:, D_MODEL:])


def _glu(x, g, w, *, tm):
    n = x.shape[0]
    row = pl.BlockSpec((tm, D_MODEL), lambda i: (i, 0))
    return pl.pallas_call(
        _glu_kernel,
        grid=(n // tm,),
        in_specs=[row, _const_spec(g.shape), _const_spec(w.shape)],
        out_specs=row,
        out_shape=jax.ShapeDtypeStruct((n, D_MODEL), F32),
        compiler_params=pltpu.CompilerParams(
            dimension_semantics=("arbitrary",), vmem_limit_bytes=VMEM_LIMIT_BYTES),
        name="conv_glu",
    )(x, g, w)


def _ln_silu(y, g, b):
    mu = jnp.mean(y, axis=-1, keepdims=True)
    yc = y - mu
    var = jnp.mean(yc * yc, axis=-1, keepdims=True)
    yn = yc * lax.rsqrt(var + LN_EPS) * g + b
    return yn * jax.nn.sigmoid(yn)


def _conv_prompt_kernel(cur_ref, prev_ref, w_ref, b_ref, lng_ref, lnb_ref, out_ref, ext_ref, sh_ref,
                        *, tm, tiles_per_seq):
    i = pl.program_id(0)
    keep = jnp.where(i % tiles_per_seq == 0, 0.0, 1.0)
    ext_ref[0:CONV_HALO, :] = prev_ref[...] * keep
    ext_ref[CONV_HALO:CONV_HALO + tm, :] = cur_ref[...]
    first_tap = CONV_HALO - (CONV_WIDTH - 1)
    n_groups = -(-CONV_WIDTH // SUBLANES)
    for b in range(SUBLANES):
        rows = tm + SUBLANES * ((CONV_WIDTH - 1 - b) // SUBLANES)
        sh_ref[b, 0:rows, :] = ext_ref[first_tap + b:first_tap + b + rows, :]

    def chunk(c, carry):
        r0 = pl.multiple_of(c * CONV_ROWS, CONV_ROWS)
        acc = jnp.zeros((CONV_ROWS, D_MODEL), F32)
        for b in range(SUBLANES):
            for a in range(n_groups):
                k = SUBLANES * a + b
                if k < CONV_WIDTH:
                    acc = acc + w_ref[k:k + 1, :] * sh_ref[b, pl.ds(r0 + SUBLANES * a, CONV_ROWS), :]
        y = _ln_silu(acc + b_ref[...], lng_ref[...], lnb_ref[...])
        out_ref[pl.ds(r0, CONV_ROWS), :] = y.astype(out_ref.dtype)
        return carry

    lax.fori_loop(0, tm // CONV_ROWS, chunk, 0)


def _conv_prompt(g, w, b, lng, lnb, *, tm, seq):
    n = g.shape[0]
    halo_per_tile = tm // CONV_HALO
    cur = pl.BlockSpec((tm, D_MODEL), lambda i: (i, 0))
    prev = pl.BlockSpec((CONV_HALO, D_MODEL), lambda i: (jnp.maximum(i * halo_per_tile - 1, 0), 0))
    return pl.pallas_call(
        functools.partial(_conv_prompt_kernel, tm=tm, tiles_per_seq=seq // tm),
        grid=(n // tm,),
        in_specs=[cur, prev] + [_const_spec(a.shape) for a in (w, b, lng, lnb)],
        out_specs=cur,
        out_shape=jax.ShapeDtypeStruct((n, D_MODEL), BF16),
        scratch_shapes=[pltpu.VMEM((CONV_HALO + tm, D_MODEL), F32),
                        pltpu.VMEM((SUBLANES, tm + CONV_HALO - SUBLANES, D_MODEL), F32)],
        compiler_params=pltpu.CompilerParams(
            dimension_semantics=("arbitrary",), vmem_limit_bytes=VMEM_LIMIT_BYTES),
        name="conv_dw_prompt",
    )(g, g, w, b, lng, lnb)


def _conv_sample_kernel(state_ref, g_ref, w_ref, b_ref, lng_ref, lnb_ref, out_ref):
    acc = w_ref[CONV_WIDTH - 1:CONV_WIDTH, :] * g_ref[...]
    for k in range(CONV_WIDTH - 1):
        acc = acc + w_ref[k:k + 1, :] * state_ref[k]
    out_ref[...] = _ln_silu(acc + b_ref[...], lng_ref[...], lnb_ref[...]).astype(out_ref.dtype)


def _conv_sample(state, g, w, b, lng, lnb, *, tb):
    n = g.shape[0]
    row = pl.BlockSpec((tb, D_MODEL), lambda i: (i, 0))
    return pl.pallas_call(
        _conv_sample_kernel,
        grid=(n // tb,),
        in_specs=[pl.BlockSpec((CONV_WIDTH - 1, tb, D_MODEL), lambda i: (0, i, 0)), row]
        + [_const_spec(a.shape) for a in (w, b, lng, lnb)],
        out_specs=row,
        out_shape=jax.ShapeDtypeStruct((n, D_MODEL), BF16),
        compiler_params=pltpu.CompilerParams(
            dimension_semantics=("arbitrary",), vmem_limit_bytes=VMEM_LIMIT_BYTES),
        name="conv_dw_sample",
    )(state, g, w, b, lng, lnb)


def kernel(x_prompt, x_sample, cache_k, cache_v, state_conv, page_table, p_prompt, p_sample, norm_ffn1, ffn1_wg, ffn1_wu, ffn1_wd, norm_mix, attn_w_qkv, attn_q_norm, attn_k_norm, attn_logit_bias, attn_w_o, conv_w_in, conv_w_dw, conv_b_dw, conv_ln_g, conv_ln_b, conv_w_out, norm_ffn2, ffn2_wg, ffn2_wu, ffn2_wd, norm_ple, ple_w_gate, ple_w_proj):
    bp, tp, _ = x_prompt.shape
    bs, ts, _ = x_sample.shape
    assert ts == 1
    n_phys = cache_k.shape[1]
    tm_p, tm_s = 512, bs

    vec = lambda a: a.reshape(1, -1)
    bf = lambda a: a.astype(BF16)
    ffn1 = [(vec(norm_ffn1[i]), bf(ffn1_wg[i]), bf(ffn1_wu[i]), bf(ffn1_wd[i])) for i in range(2)]
    ffn2 = [(vec(norm_ffn2[i]), bf(ffn2_wg[i]), bf(ffn2_wu[i]), bf(ffn2_wd[i])) for i in range(2)]
    ple_w = [(vec(norm_ple[i]), bf(ple_w_gate[i]), bf(ple_w_proj[i])) for i in range(2)]
    w_qkv, w_o = bf(attn_w_qkv[0]), bf(attn_w_o[0])
    qg = jnp.tile(attn_q_norm[0], N_HEADS).reshape(1, D_MODEL)
    kg = jnp.tile(attn_k_norm[0], N_HEADS).reshape(1, D_MODEL)
    bias = attn_logit_bias[0] * LOG2E
    w_in, w_out = bf(conv_w_in[0]), bf(conv_w_out[0])
    w_dw, b_dw = conv_w_dw[0], vec(conv_b_dw[0])
    ln_g, ln_b = vec(conv_ln_g[0]), vec(conv_ln_b[0])

    xp = x_prompt.reshape(bp * tp, D_MODEL)
    xs = x_sample.reshape(bs, D_MODEL)
    pp = p_prompt.reshape(2, bp * tp, D_PLE)
    ps = p_sample.reshape(2, bs, D_PLE)

    xp = _block(xp, ffn1[0], tm=tm_p)
    xs = _block(xs, ffn1[0], tm=tm_s)
    qp, kp, vp, kpb, vpb = _qkv(xp, vec(norm_mix[0]), w_qkv, qg, kg, tm=tm_p)
    qs, ks, vs, _, _ = _qkv(xs, vec(norm_mix[0]), w_qkv, qg, kg, tm=tm_s)
    op = _attn_prompt(qp, kpb, vpb, bias, batch=bp, seq=tp)
    os_ = _attn_decode(qs, jnp.transpose(cache_k[0], (0, 2, 3, 1)),
                       jnp.transpose(cache_v[0], (0, 2, 3, 1)), page_table, bias)
    xp = _block(xp, ffn2[0], pro=(op, w_o), epi=(pp[0],) + ple_w[0], tm=tm_p)
    xs = _block(xs, ffn2[0], pro=(os_, w_o), epi=(ps[0],) + ple_w[0], tm=tm_s)

    xp = _block(xp, ffn1[1], tm=tm_p)
    xs = _block(xs, ffn1[1], tm=tm_s)
    gp = _glu(xp, vec(norm_mix[1]), w_in, tm=tm_p)
    gs = _glu(xs, vec(norm_mix[1]), w_in, tm=tm_s)
    cp = _conv_prompt(gp, w_dw, b_dw, ln_g, ln_b, tm=tm_p, seq=tp)
    state_t = jnp.transpose(state_conv[0], (1, 0, 2))
    cs = _conv_sample(state_t, gs, w_dw, b_dw, ln_g, ln_b, tb=32)
    xp = _block(xp, ffn2[1], pro=(cp, w_out), epi=(pp[1],) + ple_w[1], tm=tm_p)
    xs = _block(xs, ffn2[1], pro=(cs, w_out), epi=(ps[1],) + ple_w[1], tm=tm_s)

    hist = CONV_WIDTH - 1
    new_conv_prompt = gp.reshape(bp, tp, D_MODEL)[:, tp - hist:]
    new_conv_sample = jnp.transpose(jnp.concatenate([state_t[1:], gs[None]], axis=0), (1, 0, 2))
    return (xp.reshape(bp, tp, D_MODEL), xs.reshape(bs, ts, D_MODEL),
            kp.reshape(1, bp, tp, N_HEADS, HEAD_DIM), vp.reshape(1, bp, tp, N_HEADS, HEAD_DIM),
            ks.reshape(1, bs, ts, N_HEADS, HEAD_DIM), vs.reshape(1, bs, ts, N_HEADS, HEAD_DIM),
            new_conv_prompt[None], new_conv_sample[None])
```
